```python
import jax, jax.numpy as jnp
from jax import lax
import numpy as np

D_MODEL = 1024
BATCH = 8
SEQ = 2048
DEPTH = 4
DEC_BATCH = 128
DEC_SEQ = 8
PAST_LEN = 2048
PAGE_SIZE = 128

N_HEADS = 8
KV_HEADS = 2
GROUP = N_HEADS // KV_HEADS
HEAD_DIM = 64
NSA_W = N_HEADS * HEAD_DIM
KV_W = KV_HEADS * HEAD_DIM
CMP_BLOCK = 32
SEL_BLOCK = 64
TOP_K = 8
WINDOW = 512
Q_BLOCK = 128
LRU_W = D_MODEL // 2
LRU_BLOCKS = 8
LRU_BW = LRU_W // LRU_BLOCKS
LRU_CONV = 4
LRU_C = 8.0
CMF_W = D_MODEL // 2
CMF_CONV = 31
D_FF = 2816
N_EXPERTS = 8
TOP_E = 2
D_EXPERT = 3584
N_DENSE = (DEPTH + 1) // 2
N_MOE = DEPTH // 2
N_BRANCH = 3
EPS = 1e-6
NEG_INF = -1e30
SPLITS = (NSA_W, 4 * KV_W, 2 * KV_W, N_BRANCH * N_HEADS, LRU_W, LRU_W, 2 * CMF_W, N_BRANCH * D_MODEL)
D_IN = sum(SPLITS)

kernel_name = 'hybrid_nsa_rglru_conformer_decoder_step'


def rmsnorm(x, g):
    xf = x.astype(jnp.float32)
    y = xf * lax.rsqrt(jnp.mean(xf * xf, axis=-1, keepdims=True) + EPS)
    return y.astype(x.dtype) * g


def layernorm(x, g, b):
    xf = x.astype(jnp.float32)
    mu = jnp.mean(xf, axis=-1, keepdims=True)
    var = jnp.mean(jnp.square(xf - mu), axis=-1, keepdims=True)
    return ((xf - mu) * lax.rsqrt(var + EPS)).astype(x.dtype) * g + b


def masked_softmax(s, mask):
    s = jnp.where(mask, s, NEG_INF)
    m = jnp.max(s, axis=-1, keepdims=True)
    p = jnp.exp(s - m) * mask
    return p / jnp.maximum(jnp.sum(p, axis=-1, keepdims=True), 1e-30)


def alibi_slopes():
    m = 2.0 ** (-8.0 * (jnp.arange(N_HEADS, dtype=jnp.float32) + 1.0) / N_HEADS)
    return m.reshape(KV_HEADS, GROUP)


def causal_dwconv(x, hist, w, b):
    xp = jnp.concatenate([hist.astype(x.dtype), x], axis=1)
    y = lax.conv_general_dilated(xp, w[:, None, :].astype(x.dtype), (1,), 'VALID',
                                 dimension_numbers=('NWC', 'WIO', 'NWC'),
                                 feature_group_count=x.shape[-1])
    return y + b, xp[:, -(w.shape[0] - 1):]


def nsa_key_blocks(kv4, w_ck, w_cv):
    N, T = kv4.shape[:2]
    t_pad = -(-T // SEL_BLOCK) * SEL_BLOCK
    kv4 = jnp.pad(kv4, ((0, 0), (0, t_pad - T), (0, 0), (0, 0), (0, 0)))
    cb = kv4.reshape(N, t_pad // CMP_BLOCK, CMP_BLOCK, 4, KV_HEADS, HEAD_DIM)
    kc = jnp.einsum('ncjhe,je->nche', cb[:, :, :, 0], w_ck.astype(kv4.dtype))
    vc = jnp.einsum('ncjhe,je->nche', cb[:, :, :, 1], w_cv.astype(kv4.dtype))
    sb = kv4[:, :, 2:].reshape(N, t_pad // SEL_BLOCK, SEL_BLOCK, 2, KV_HEADS, HEAD_DIM)
    sb = sb.transpose(3, 0, 4, 1, 2, 5)
    return kc, vc, sb[0], sb[1]


def nsa_attend(q, q_pos, kc, vc, ks, vs, kw, vw, kw_pos, gates, slopes):
    f32 = jnp.float32
    N, Q = q.shape[:2]
    scale = HEAD_DIM ** -0.5
    sl = slopes[None, None, :, :, None]
    n_cmp = kc.shape[1]
    c_end = (jnp.arange(n_cmp) + 1) * CMP_BLOCK - 1
    d_c = q_pos[:, None] - c_end[None, :]
    s_c = jnp.einsum('nqhge,nche->nqhgc', q, kc).astype(f32) * scale - sl * d_c[None, :, None, None, :].astype(f32)
    p_c = masked_softmax(s_c, (d_c >= 0)[None, :, None, None, :])
    o_c = jnp.einsum('nqhgc,nche->nqhge', p_c.astype(vc.dtype), vc)
    n_sel = ks.shape[2]
    imp = jnp.sum(p_c, axis=3).reshape(N, Q, KV_HEADS, n_sel, SEL_BLOCK // CMP_BLOCK).sum(-1)
    blk = jnp.arange(n_sel)
    cur = q_pos // SEL_BLOCK
    valid_b = (blk[None, :] <= cur[:, None])[None, :, None, :]
    forced = ((blk[None, :] == cur[:, None]) | (blk[None, :] == 0))[None, :, None, :]
    score = jnp.where(valid_b, jnp.where(forced, 2.0 * GROUP, imp), -1.0)
    k_top = min(TOP_K, n_sel)
    top_s, idx = lax.top_k(score, k_top)
    sel_ok = top_s >= 0.0
    n_i = jnp.arange(N)[:, None, None, None]
    h_i = jnp.arange(KV_HEADS)[None, None, :, None]
    kg = ks[n_i, h_i, idx]
    vg = vs[n_i, h_i, idx]
    pos_s = idx[..., None] * SEL_BLOCK + jnp.arange(SEL_BLOCK)
    d_s = q_pos[None, :, None, None, None] - pos_s
    m_s = (sel_ok[..., None] & (d_s >= 0)).reshape(N, Q, KV_HEADS, 1, k_top * SEL_BLOCK)
    s_s = jnp.einsum('nqhge,nqhkbe->nqhgkb', q, kg).astype(f32).reshape(N, Q, KV_HEADS, GROUP, k_top * SEL_BLOCK)
    s_s = s_s * scale - sl * d_s.reshape(N, Q, KV_HEADS, 1, k_top * SEL_BLOCK).astype(f32)
    p_s = masked_softmax(s_s, m_s)
    o_s = jnp.einsum('nqhgj,nqhje->nqhge', p_s.astype(vs.dtype),
                     vg.reshape(N, Q, KV_HEADS, k_top * SEL_BLOCK, HEAD_DIM))
    d_w = q_pos[:, None] - kw_pos[None, :]
    m_w = ((d_w >= 0) & (d_w <= WINDOW) & (kw_pos[None, :] >= 0))[None, :, None, None, :]
    s_w = jnp.einsum('nqhge,nwhe->nqhgw', q, kw).astype(f32) * scale - sl * d_w[None, :, None, None, :].astype(f32)
    p_w = masked_softmax(s_w, m_w)
    o_w = jnp.einsum('nqhgw,nwhe->nqhge', p_w.astype(vw.dtype), vw)
    g = jax.nn.sigmoid(gates.astype(f32)).astype(q.dtype)
    return g[..., 0:1] * o_c + g[..., 1:2] * o_s + g[..., 2:3] * o_w


def nsa_prompt(q, kv4, kvw, gates, w_ck, w_cv, slopes):
    B, S = q.shape[:2]
    kc, vc, ks, vs = nsa_key_blocks(kv4, w_ck, w_cv)
    qb = min(Q_BLOCK, S)
    kw_pad = jnp.pad(kvw, ((0, 0), (WINDOW, 0), (0, 0), (0, 0), (0, 0)))

    def one_block(i):
        st = i * qb
        q_i = lax.dynamic_slice_in_dim(q, st, qb, axis=1)
        g_i = lax.dynamic_slice_in_dim(gates, st, qb, axis=1)
        w_i = lax.dynamic_slice_in_dim(kw_pad, st, WINDOW + qb, axis=1)
        q_pos = st + jnp.arange(qb)
        kw_pos = st - WINDOW + jnp.arange(WINDOW + qb)
        return nsa_attend(q_i, q_pos, kc, vc, ks, vs, w_i[:, :, 0], w_i[:, :, 1], kw_pos, g_i, slopes)

    out = lax.map(one_block, jnp.arange(S // qb))
    return out.transpose(1, 0, 2, 3, 4, 5).reshape(B, S, NSA_W)


def nsa_sample(q, kv4_new, kvw_new, gates, kv4_past, win_buf, w_ck, w_cv, slopes):
    N, T = q.shape[:2]
    P = kv4_past.shape[1]
    WB = win_buf.shape[1]
    kv4 = jnp.concatenate([kv4_past.astype(kv4_new.dtype), kv4_new], axis=1)
    kc, vc, ks, vs = nsa_key_blocks(kv4, w_ck, w_cv)
    kw = jnp.concatenate([win_buf.astype(kvw_new.dtype), kvw_new], axis=1)
    q_pos = P + jnp.arange(T)
    kw_pos = P - WB + jnp.arange(WB + T)
    out = nsa_attend(q, q_pos, kc, vc, ks, vs, kw[:, :, 0], kw[:, :, 1], kw_pos, gates, slopes)
    return out.reshape(N, T, NSA_W), kw[:, -WB:]


def rglru(x_in, gate_in, hist, h0, conv_w, conv_b, w_a, b_a, w_x, b_x, lam):
    f32 = jnp.float32
    N, T, _ = x_in.shape
    xc, hist_new = causal_dwconv(x_in, hist, conv_w, conv_b)
    xf = xc.astype(f32)
    xb = xf.reshape(N, T, LRU_BLOCKS, LRU_BW)
    r = jax.nn.sigmoid(jnp.einsum('ntkc,kcd->ntkd', xb, w_a.astype(f32)).reshape(N, T, LRU_W) + b_a.astype(f32))
    i = jax.nn.sigmoid(jnp.einsum('ntkc,kcd->ntkd', xb, w_x.astype(f32)).reshape(N, T, LRU_W) + b_x.astype(f32))
    log_a = -LRU_C * r * jax.nn.softplus(-lam.astype(f32))
    a = jnp.exp(log_a)
    b = jnp.sqrt(-jnp.expm1(2.0 * log_a)) * (i * xf)

    def step(hc, ab):
        hc = ab[0] * hc + ab[1]
        return hc, hc

    h_last, hs = lax.scan(step, h0.astype(f32), (jnp.swapaxes(a, 0, 1), jnp.swapaxes(b, 0, 1)))
    y = jnp.swapaxes(hs, 0, 1) * jax.nn.gelu(gate_in.astype(f32))
    return y.astype(x_in.dtype), h_last, hist_new


def conformer_conv(z, hist, dw_w, dw_b, ln_g, ln_b):
    a, gte = jnp.split(z, 2, axis=-1)
    u = a * jax.nn.sigmoid(gte)
    c, hist_new = causal_dwconv(u, hist, dw_w, dw_b)
    c = layernorm(c, ln_g, ln_b)
    return c * jax.nn.sigmoid(c), hist_new


def swiglu(v, wg, wu, wd):
    return (jax.nn.silu(v @ wg) * (v @ wu)) @ wd


def moe_swiglu(v, w_router, wg, wu, wd):
    logits = (v @ w_router).astype(jnp.float32)
    top_l, top_i = lax.top_k(logits, TOP_E)
    wts = jax.nn.softmax(top_l, axis=-1)
    combine = jnp.sum(jax.nn.one_hot(top_i, N_EXPERTS, dtype=jnp.float32) * wts[..., None], axis=-2).astype(v.dtype)
    out = jnp.zeros_like(v)
    for e in range(N_EXPERTS):
        out = out + combine[:, e:e + 1] * swiglu(v, wg[e], wu[e], wd[e])
    return out


def trunk_layer(h, l, P, slopes, past):
    N, T, _ = h.shape
    dt = h.dtype
    u = rmsnorm(h, P['g_mix'][l])
    z = u @ P['w_in'][l]
    cuts = [int(c) for c in np.cumsum(SPLITS)[:-1]]
    q, kv4, kvw, g_nsa, lru_x, lru_g, cmf_in, g_mrg = jnp.split(z, cuts, axis=-1)
    q = q.reshape(N, T, KV_HEADS, GROUP, HEAD_DIM)
    kv4 = kv4.reshape(N, T, 4, KV_HEADS, HEAD_DIM)
    kvw = kvw.reshape(N, T, 2, KV_HEADS, HEAD_DIM)
    g_nsa = g_nsa.reshape(N, T, KV_HEADS, GROUP, N_BRANCH)
    w_ck, w_cv = P['w_cmp_k'][l], P['w_cmp_v'][l]
    if past is None:
        o_nsa = nsa_prompt(q, kv4, kvw, g_nsa, w_ck, w_cv, slopes)
        new_win = kvw[:, -min(WINDOW, T):]
        h0 = jnp.zeros((N, LRU_W), jnp.float32)
        lru_hist = jnp.zeros((N, LRU_CONV - 1, LRU_W), dt)
        cmf_hist = jnp.zeros((N, CMF_CONV - 1, CMF_W), dt)
    else:
        kv4_past, win_buf, h0, lru_hist, cmf_hist = past
        o_nsa, new_win = nsa_sample(q, kv4, kvw, g_nsa, kv4_past, win_buf, w_ck, w_cv, slopes)
    y_lru, h_last, lru_hist_new = rglru(lru_x, lru_g, lru_hist, h0, P['lru_conv_w'][l], P['lru_conv_b'][l],
                                        P['lru_w_a'][l], P['lru_b_a'][l], P['lru_w_x'][l], P['lru_b_x'][l],
                                        P['lru_lambda'][l])
    y_cmf, cmf_hist_new = conformer_conv(cmf_in, cmf_hist, P['cmf_dw_w'][l], P['cmf_dw_b'][l],
                                         P['cmf_ln_g'][l], P['cmf_ln_b'][l])
    gm = jax.nn.sigmoid(g_mrg.astype(jnp.float32)).astype(dt).reshape(N, T, N_BRANCH, D_MODEL)
    merged = (gm[:, :, 0] * (o_nsa @ P['w_br_nsa'][l])
              + gm[:, :, 1] * (y_lru @ P['w_br_lru'][l])
              + gm[:, :, 2] * (y_cmf @ P['w_br_cmf'][l]))
    h = h + merged @ P['w_out'][l]
    v = rmsnorm(h, P['g_ffn'][l]).reshape(N * T, D_MODEL)
    j = l // 2
    if l % 2 == 0:
        f = swiglu(v, P['ffn_w_gate'][j], P['ffn_w_up'][j], P['ffn_w_down'][j])
    else:
        f = moe_swiglu(v, P['moe_router'][j], P['moe_w_gate'][j], P['moe_w_up'][j], P['moe_w_down'][j])
    h = h + f.reshape(N, T, D_MODEL)
    return h, (kv4, new_win, h_last.astype(dt), lru_hist_new, cmf_hist_new)


def _stack(outs, j):
    return jnp.stack([o[j] for o in outs])


def setup_inputs(seed: int = 0) -> dict:
    key = jax.random.key(seed)
    keys = iter(jax.random.split(key, 48))
    f32 = jnp.float32

    def nrm(shape, scale):
        return jax.random.normal(next(keys), shape, f32) * scale

    def gain(shape):
        return 1.0 + nrm(shape, 0.05)

    n_pages = PAST_LEN // PAGE_SIZE
    n_used = DEC_BATCH * n_pages
    n_pool = n_used + (n_used + 3) // 4
    w_buf = min(WINDOW, PAST_LEN)
    x_prompt = nrm((BATCH, SEQ, D_MODEL), 1.0)
    x_sample = nrm((DEC_BATCH, DEC_SEQ, D_MODEL), 1.0)
    cache_nsa_kv = nrm((DEPTH, n_pool, PAGE_SIZE, 4, KV_HEADS, HEAD_DIM), 1.0)
    cache_win_kv = nrm((DEPTH, DEC_BATCH, w_buf, 2, KV_HEADS, HEAD_DIM), 1.0)
    state_lru_h = nrm((DEPTH, DEC_BATCH, LRU_W), 0.5)
    state_lru_conv = nrm((DEPTH, DEC_BATCH, LRU_CONV - 1, LRU_W), 1.0)
    state_cmf_conv = nrm((DEPTH, DEC_BATCH, CMF_CONV - 1, CMF_W), 0.5)
    page_table = jax.random.permutation(next(keys), n_pool)[:n_used].reshape(DEC_BATCH, n_pages).astype(jnp.int32)
    a_c = jax.random.uniform(next(keys), (DEPTH, LRU_W), f32, 0.9, 0.999) ** (1.0 / LRU_C)
    lru_lambda = jnp.log(a_c) - jnp.log1p(-a_c)
    return {
        'x_prompt': x_prompt,
        'x_sample': x_sample,
        'cache_nsa_kv': cache_nsa_kv,
        'cache_win_kv': cache_win_kv,
        'state_lru_h': state_lru_h,
        'state_lru_conv': state_lru_conv,
        'state_cmf_conv': state_cmf_conv,
        'page_table': page_table,
        'g_mix': gain((DEPTH, D_MODEL)),
        'w_in': nrm((DEPTH, D_MODEL, D_IN), D_MODEL ** -0.5),
        'w_cmp_k': (1.0 + nrm((DEPTH, CMP_BLOCK, HEAD_DIM), 0.1)) / CMP_BLOCK,
        'w_cmp_v': (1.0 + nrm((DEPTH, CMP_BLOCK, HEAD_DIM), 0.1)) / CMP_BLOCK,
        'lru_conv_w': nrm((DEPTH, LRU_CONV, LRU_W), LRU_CONV ** -0.5),
        'lru_conv_b': nrm((DEPTH, LRU_W), 0.01),
        'lru_w_a': nrm((DEPTH, LRU_BLOCKS, LRU_BW, LRU_BW), LRU_BW ** -0.5),
        'lru_b_a': nrm((DEPTH, LRU_W), 0.01),
        'lru_w_x': nrm((DEPTH, LRU_BLOCKS, LRU_BW, LRU_BW), LRU_BW ** -0.5),
        'lru_b_x': nrm((DEPTH, LRU_W), 0.01),
        'lru_lambda': lru_lambda,
        'cmf_dw_w': nrm((DEPTH, CMF_CONV, CMF_W), CMF_CONV ** -0.5),
        'cmf_dw_b': nrm((DEPTH, CMF_W), 0.01),
        'cmf_ln_g': gain((DEPTH, CMF_W)),
        'cmf_ln_b': nrm((DEPTH, CMF_W), 0.01),
        'w_br_nsa': nrm((DEPTH, NSA_W, D_MODEL), NSA_W ** -0.5),
        'w_br_lru': nrm((DEPTH, LRU_W, D_MODEL), LRU_W ** -0.5),
        'w_br_cmf': nrm((DEPTH, CMF_W, D_MODEL), CMF_W ** -0.5),
        'w_out': nrm((DEPTH, D_MODEL, D_MODEL), D_MODEL ** -0.5),
        'g_ffn': gain((DEPTH, D_MODEL)),
        'ffn_w_gate': nrm((N_DENSE, D_MODEL, D_FF), D_MODEL ** -0.5),
        'ffn_w_up': nrm((N_DENSE, D_MODEL, D_FF), D_MODEL ** -0.5),
        'ffn_w_down': nrm((N_DENSE, D_FF, D_MODEL), D_FF ** -0.5),
        'moe_router': nrm((N_MOE, D_MODEL, N_EXPERTS), D_MODEL ** -0.5),
        'moe_w_gate': nrm((N_MOE, N_EXPERTS, D_MODEL, D_EXPERT), D_MODEL ** -0.5),
        'moe_w_up': nrm((N_MOE, N_EXPERTS, D_MODEL, D_EXPERT), D_MODEL ** -0.5),
        'moe_w_down': nrm((N_MOE, N_EXPERTS, D_EXPERT, D_MODEL), D_EXPERT ** -0.5),
        'g_final': gain((D_MODEL,)),
    }


def reference(x_prompt, x_sample, cache_nsa_kv, cache_win_kv, state_lru_h, state_lru_conv, state_cmf_conv,
              page_table, g_mix, w_in, w_cmp_k, w_cmp_v, lru_conv_w, lru_conv_b, lru_w_a, lru_b_a, lru_w_x,
              lru_b_x, lru_lambda, cmf_dw_w, cmf_dw_b, cmf_ln_g, cmf_ln_b, w_br_nsa, w_br_lru, w_br_cmf, w_out,
              g_ffn, ffn_w_gate, ffn_w_up, ffn_w_down, moe_router, moe_w_gate, moe_w_up, moe_w_down, g_final):
    P = {'g_mix': g_mix, 'w_in': w_in, 'w_cmp_k': w_cmp_k, 'w_cmp_v': w_cmp_v,
         'lru_conv_w': lru_conv_w, 'lru_conv_b': lru_conv_b, 'lru_w_a': lru_w_a, 'lru_b_a': lru_b_a,
         'lru_w_x': lru_w_x, 'lru_b_x': lru_b_x, 'lru_lambda': lru_lambda,
         'cmf_dw_w': cmf_dw_w, 'cmf_dw_b': cmf_dw_b, 'cmf_ln_g': cmf_ln_g, 'cmf_ln_b': cmf_ln_b,
         'w_br_nsa': w_br_nsa, 'w_br_lru': w_br_lru, 'w_br_cmf': w_br_cmf, 'w_out': w_out, 'g_ffn': g_ffn,
         'ffn_w_gate': ffn_w_gate, 'ffn_w_up': ffn_w_up, 'ffn_w_down': ffn_w_down,
         'moe_router': moe_router, 'moe_w_gate': moe_w_gate, 'moe_w_up': moe_w_up, 'moe_w_down': moe_w_down}
    slopes = alibi_slopes()
    n_seq, n_pages = page_table.shape
    hp, hs = x_prompt, x_sample
    outs_p, outs_s = [], []
    for l in range(DEPTH):
        hp, st_p = trunk_layer(hp, l, P, slopes, None)
        pool = cache_nsa_kv[l]
        kv_past = pool[page_table].reshape(n_seq, n_pages * pool.shape[1], 4, KV_HEADS, HEAD_DIM)
        past = (kv_past, cache_win_kv[l], state_lru_h[l], state_lru_conv[l], state_cmf_conv[l])
        hs, st_s = trunk_layer(hs, l, P, slopes, past)
        outs_p.append(st_p)
        outs_s.append(st_s)
    y_prompt = rmsnorm(hp, g_final)
    y_sample = rmsnorm(hs, g_final)
    return (y_prompt, y_sample,
            _stack(outs_p, 0), _stack(outs_s, 0),
            _stack(outs_p, 1), _stack(outs_s, 1),
            _stack(outs_p, 2), _stack(outs_s, 2),
            _stack(outs_p, 3), _stack(outs_s, 3),
            _stack(outs_p, 4), _stack(outs_s, 4))
```

```python
import functools

import jax
import jax.numpy as jnp
import numpy as np
from jax import lax
from jax.experimental import pallas as pl
from jax.experimental.pallas import tpu as pltpu

F32 = jnp.float32
BF16 = jnp.bfloat16
I32 = jnp.int32

D_MODEL = 1024
N_HEADS = 8
KV_HEADS = 2
GROUP = N_HEADS // KV_HEADS
HEAD_DIM = 64
NSA_W = N_HEADS * HEAD_DIM
KV_W = KV_HEADS * HEAD_DIM
CMP_BLOCK = 32
SEL_BLOCK = 64
TOP_K = 8
WINDOW = 512
Q_BLOCK = 128
LRU_W = D_MODEL // 2
LRU_BLOCKS = 8
LRU_BW = LRU_W // LRU_BLOCKS
LRU_CONV = 4
LRU_C = 8.0
CMF_W = D_MODEL // 2
CMF_CONV = 31
N_EXPERTS = 8
TOP_E = 2
N_BRANCH = 3
EPS = 1e-6
NEG_INF = -1e30
MASKED_BELOW = -1e29

LANES = 128
SUBLANES = 8
VMEM_LIMIT = 56 * 1024 * 1024

C_Q = 0
C_KV4 = 512
C_LRUX = 1024
C_LRUG = 1536
C_CMFA = 2048
C_CMFG = 2560
C_GM = 3072
C_KVW = 6144
C_GNSA = 6400
N_Z = 6656
KEY_CHUNK = 128


def _log2(n):
    assert n & (n - 1) == 0 and n > 0
    return n.bit_length() - 1


def _cparams(sem):
    return pltpu.CompilerParams(dimension_semantics=sem, vmem_limit_bytes=VMEM_LIMIT)


def _sigmoid(x):
    return 1.0 / (1.0 + jnp.exp(-x))


def _rms(x, g):
    return x * lax.rsqrt(jnp.mean(x * x, axis=-1, keepdims=True) + EPS) * g


def _inproj_kernel(h_ref, g_ref, w_ref, z_ref, u_scr):
    @pl.when(pl.program_id(1) == 0)
    def _():
        u_scr[...] = _rms(h_ref[...], g_ref[...]).astype(BF16)

    z_ref[...] = jnp.dot(u_scr[...], w_ref[...], preferred_element_type=F32)


def _inproj(h, g, w, tm, tn):
    m, d = h.shape
    n = w.shape[1]
    return pl.pallas_call(
        _inproj_kernel,
        out_shape=jax.ShapeDtypeStruct((m, n), F32),
        grid=(m // tm, n // tn),
        in_specs=[pl.BlockSpec((tm, d), lambda i, j: (i, 0)),
                  pl.BlockSpec((1, d), lambda i, j: (0, 0)),
                  pl.BlockSpec((d, tn), lambda i, j: (0, j))],
        out_specs=pl.BlockSpec((tm, tn), lambda i, j: (i, j)),
        scratch_shapes=[pltpu.VMEM((tm, d), BF16)],
        compiler_params=_cparams(("parallel", "arbitrary")),
        name="inproj",
    )(h, g, w)


def _head_rows(q, h, scale):
    parts = [q[:, (h * GROUP + g) * HEAD_DIM:(h * GROUP + g + 1) * HEAD_DIM] for g in range(GROUP)]
    return (jnp.concatenate(parts, axis=0) * scale).astype(BF16)


def _row_slopes(h, t):
    g = lax.broadcasted_iota(I32, (GROUP * t, 1), 0) >> _log2(t)
    s = jnp.full((GROUP * t, 1), 2.0 ** (-(h * GROUP + GROUP)), F32)
    for gi in range(GROUP - 1):
        s = jnp.where(g == gi, 2.0 ** (-(h * GROUP + gi + 1)), s)
    return s


def _compress(kv, w):
    n = kv.shape[0] // CMP_BLOCK
    return jnp.sum(kv.reshape(n, CMP_BLOCK, kv.shape[1]) * w[None], axis=1)


def _qk(q_rows, k):
    return lax.dot_general(q_rows, k, (((1,), (1,)), ((), ())), preferred_element_type=F32)


def _cmp_branch(q_rows, kc, vc, slope, qpos_rows, n_cmp):
    r = q_rows.shape[0]
    j = lax.broadcasted_iota(I32, (r, LANES), 1)
    d = qpos_rows - ((j + 1) * CMP_BLOCK - 1)
    mask = (d >= 0) & (j < n_cmp)
    s = _qk(q_rows, kc) - slope * d.astype(F32)
    s = jnp.where(mask, s, NEG_INF)
    m = jnp.max(s, axis=-1, keepdims=True)
    p = jnp.where(mask, jnp.exp(s - m), 0.0)
    p = p / jnp.maximum(jnp.sum(p, axis=-1, keepdims=True), 1e-30)
    o = jnp.dot(p.astype(BF16), vc, preferred_element_type=F32)
    return o, p


def _select_blocks(p_c, qpos_q, t, n_sel):
    pcs = p_c[0:t]
    for g in range(1, GROUP):
        pcs = pcs + p_c[g * t:(g + 1) * t]
    imp = pcs + pltpu.roll(pcs, LANES - 1, axis=1)
    col = lax.broadcasted_iota(I32, (t, LANES), 1)
    s_idx = col >> 1
    live = ((col & 1) == 0) & (col < 2 * n_sel)
    cur = qpos_q >> _log2(SEL_BLOCK)
    valid = s_idx <= cur
    forced = (s_idx == cur) | (s_idx == 0)
    score = jnp.where(live, jnp.where(valid, jnp.where(forced, 2.0 * GROUP, imp), -1.0), -2.0)
    rank = jnp.zeros((t, LANES), F32)
    for jp in range(0, 2 * n_sel, 2):
        cj = score[:, jp:jp + 1]
        before = (cj > score) | ((cj == score) & (col > jp))
        rank = rank + jnp.where(before, 1.0, 0.0)
    k_top = min(TOP_K, n_sel)
    return jnp.where((rank < k_top) & (score >= 0.0), 1.0, 0.0)


def _expand_sel(sel_bf, first_block, t):
    j = lax.broadcasted_iota(I32, (LANES, KEY_CHUNK), 0)
    k = lax.broadcasted_iota(I32, (LANES, KEY_CHUNK), 1)
    e = jnp.where(j == 2 * (first_block + (k >> _log2(SEL_BLOCK))), 1.0, 0.0).astype(BF16)
    m = jnp.dot(sel_bf, e, preferred_element_type=F32)
    return jnp.concatenate([m] * GROUP, axis=0)


def _finish_softmax(s_scr, n_chunks_static, m_lane, v_of_chunk, rows):
    m = jnp.max(m_lane, axis=-1, keepdims=True)
    l_lane = jnp.zeros((rows, KEY_CHUNK), F32)
    acc = jnp.zeros((rows, HEAD_DIM), F32)
    for c in range(n_chunks_static):
        s = s_scr[c]
        p = jnp.where(s > MASKED_BELOW, jnp.exp(s - m), 0.0)
        l_lane = l_lane + p
        acc = acc + jnp.dot(p.astype(BF16), v_of_chunk(c), preferred_element_type=F32)
    l = jnp.sum(l_lane, axis=-1, keepdims=True)
    return acc / jnp.maximum(l, 1e-30)


def _gate_mix(gs, h, t, o_c, o_s, o_w):
    outs = []
    for g in range(GROUP):
        c0 = (h * GROUP + g) * N_BRANCH
        sl = slice(g * t, (g + 1) * t)
        outs.append(gs[:, c0:c0 + 1] * o_c[sl] + gs[:, c0 + 1:c0 + 2] * o_s[sl] + gs[:, c0 + 2:c0 + 3] * o_w[sl])
    return outs


def _nsa_prompt_kernel(q_ref, kvc_ref, kvs_ref, kvw_ref, gate_ref, wck_ref, wcv_ref, o_ref,
                       kc_scr, vc_scr, s_scr, m_scr, l_scr, acc_scr):
    i = pl.program_id(1)
    seq = kvc_ref.shape[0]
    n_cmp = seq // CMP_BLOCK
    n_sel = seq // SEL_BLOCK
    t = Q_BLOCK
    rows = GROUP * t
    n_win_chunks = WINDOW // KEY_CHUNK + 1

    @pl.when(i == 0)
    def _():
        kc_scr[...] = jnp.zeros(kc_scr.shape, BF16)
        vc_scr[...] = jnp.zeros(vc_scr.shape, BF16)
        kc_scr[0:n_cmp, :] = _compress(kvc_ref[:, 0:KV_W], wck_ref[...]).astype(BF16)
        vc_scr[0:n_cmp, :] = _compress(kvc_ref[:, KV_W:2 * KV_W], wcv_ref[...]).astype(BF16)

    q = q_ref[...]
    gs = _sigmoid(gate_ref[...])
    qpos_q = i * t + lax.broadcasted_iota(I32, (t, 1), 0)
    qpos = i * t + (lax.broadcasted_iota(I32, (rows, 1), 0) & (t - 1))
    lane = lax.broadcasted_iota(I32, (1, KEY_CHUNK), 1)
    heads = []
    for h in range(KV_HEADS):
        hs = slice(h * HEAD_DIM, (h + 1) * HEAD_DIM)
        vs = slice(KV_W + h * HEAD_DIM, KV_W + (h + 1) * HEAD_DIM)
        q_rows = _head_rows(q, h, HEAD_DIM ** -0.5)
        slope = _row_slopes(h, t)
        o_c, p_c = _cmp_branch(q_rows, kc_scr[:, hs], vc_scr[:, hs], slope, qpos, n_cmp)
        sel_bf = _select_blocks(p_c, qpos_q, t, n_sel).astype(BF16)

        m_scr[...] = jnp.full(m_scr.shape, NEG_INF, F32)

        def sel_scores(c, carry):
            start = pl.multiple_of(c * KEY_CHUNK, KEY_CHUNK)
            k_c = kvs_ref[pl.ds(start, KEY_CHUNK), hs].astype(BF16)
            d = qpos - (c * KEY_CHUNK + lane)
            flags = _expand_sel(sel_bf, 2 * c, t)
            s = _qk(q_rows, k_c) - slope * d.astype(F32)
            s = jnp.where((flags > 0.5) & (d >= 0), s, NEG_INF)
            s_scr[c] = s
            m_scr[...] = jnp.maximum(m_scr[...], s)
            return carry

        lax.fori_loop(0, i + 1, sel_scores, 0)
        m_sel = jnp.max(m_scr[...], axis=-1, keepdims=True)
        l_scr[...] = jnp.zeros(l_scr.shape, F32)
        acc_scr[...] = jnp.zeros(acc_scr.shape, F32)

        def sel_values(c, carry):
            start = pl.multiple_of(c * KEY_CHUNK, KEY_CHUNK)
            v_c = kvs_ref[pl.ds(start, KEY_CHUNK), vs].astype(BF16)
            s = s_scr[c]
            p = jnp.where(s > MASKED_BELOW, jnp.exp(s - m_sel), 0.0)
            l_scr[...] += p
            acc_scr[...] += jnp.dot(p.astype(BF16), v_c, preferred_element_type=F32)
            return carry

        lax.fori_loop(0, i + 1, sel_values, 0)
        o_s = acc_scr[...] / jnp.maximum(jnp.sum(l_scr[...], axis=-1, keepdims=True), 1e-30)

        m_lane = jnp.full((rows, KEY_CHUNK), NEG_INF, F32)
        for w in range(n_win_chunks):
            cw = i - (n_win_chunks - 1) + w
            start = pl.multiple_of(jnp.maximum(cw, 0) * KEY_CHUNK, KEY_CHUNK)
            k_c = kvw_ref[pl.ds(start, KEY_CHUNK), hs].astype(BF16)
            kpos = cw * KEY_CHUNK + lane
            d = qpos - kpos
            s = _qk(q_rows, k_c) - slope * d.astype(F32)
            s = jnp.where((d >= 0) & (d <= WINDOW) & (kpos >= 0), s, NEG_INF)
            s_scr[w] = s
            m_lane = jnp.maximum(m_lane, s)

        def win_v(w):
            cw = i - (n_win_chunks - 1) + w
            start = pl.multiple_of(jnp.maximum(cw, 0) * KEY_CHUNK, KEY_CHUNK)
            return kvw_ref[pl.ds(start, KEY_CHUNK), vs].astype(BF16)

        o_w = _finish_softmax(s_scr, n_win_chunks, m_lane, win_v, rows)
        heads.extend(_gate_mix(gs, h, t, o_c, o_s, o_w))
    o_ref[...] = jnp.concatenate(heads, axis=1).astype(o_ref.dtype)


def _nsa_prompt(z, wck2, wcv2, batch, seq):
    n_qb = seq // Q_BLOCK
    rows = GROUP * Q_BLOCK
    kvw_blk = C_KVW // (2 * KV_W)
    return pl.pallas_call(
        _nsa_prompt_kernel,
        out_shape=jax.ShapeDtypeStruct((batch * seq, NSA_W), BF16),
        grid=(batch, n_qb),
        in_specs=[pl.BlockSpec((Q_BLOCK, NSA_W), lambda n, i: (n * n_qb + i, C_Q // NSA_W)),
                  pl.BlockSpec((seq, 2 * KV_W), lambda n, i: (n, C_KV4 // (2 * KV_W))),
                  pl.BlockSpec((seq, 2 * KV_W), lambda n, i: (n, C_KV4 // (2 * KV_W) + 1)),
                  pl.BlockSpec((seq, 2 * KV_W), lambda n, i: (n, kvw_blk)),
                  pl.BlockSpec((Q_BLOCK, LANES), lambda n, i: (n * n_qb + i, C_GNSA // LANES)),
                  pl.BlockSpec((CMP_BLOCK, KV_W), lambda n, i: (0, 0)),
                  pl.BlockSpec((CMP_BLOCK, KV_W), lambda n, i: (0, 0))],
        out_specs=pl.BlockSpec((Q_BLOCK, NSA_W), lambda n, i: (n * n_qb + i, 0)),
        scratch_shapes=[pltpu.VMEM((LANES, KV_W), BF16),
                        pltpu.VMEM((LANES, KV_W), BF16),
                        pltpu.VMEM((max(n_qb, WINDOW // KEY_CHUNK + 1), rows, KEY_CHUNK), F32),
                        pltpu.VMEM((rows, KEY_CHUNK), F32),
                        pltpu.VMEM((rows, KEY_CHUNK), F32),
                        pltpu.VMEM((rows, HEAD_DIM), F32)],
        compiler_params=_cparams(("parallel", "arbitrary")),
        name="nsa_prompt",
    )(z, z, z, z, z, wck2, wcv2)


def _nsa_sample_kernel(pt_ref, *refs, n_pages, t_new):
    page_refs = refs[:n_pages]
    (q_ref, kvn_ref, kwn_ref, gate_ref, win_ref, wck_ref, wcv_ref,
     o_ref, win_out_ref, kc_scr, vc_scr, s_scr) = refs[n_pages:]
    del pt_ref
    t = t_new
    rows = GROUP * t
    past = n_pages * KEY_CHUNK
    n_cmp_past = past // CMP_BLOCK
    n_sel = -(-(past + t) // SEL_BLOCK)
    per_page = KEY_CHUNK // CMP_BLOCK
    win_len = win_ref.shape[0]
    n_win_chunks = win_len // KEY_CHUNK
    pad_rows = KEY_CHUNK - t

    kc_scr[...] = jnp.zeros(kc_scr.shape, F32)
    vc_scr[...] = jnp.zeros(vc_scr.shape, F32)
    for p in range(n_pages):
        kc_scr[p * per_page:(p + 1) * per_page, :] = _compress(page_refs[p][:, 0:KV_W], wck_ref[...])
        vc_scr[p * per_page:(p + 1) * per_page, :] = _compress(page_refs[p][:, KV_W:2 * KV_W], wcv_ref[...])
    kc = kc_scr[...].astype(BF16)
    vc = vc_scr[...].astype(BF16)

    q = q_ref[...]
    kvn = kvn_ref[...]
    kwn = kwn_ref[...]
    gs = _sigmoid(gate_ref[...])
    qpos_q = past + lax.broadcasted_iota(I32, (t, 1), 0)
    qpos = past + (lax.broadcasted_iota(I32, (rows, 1), 0) & (t - 1))
    lane = lax.broadcasted_iota(I32, (1, KEY_CHUNK), 1)
    zpad = jnp.zeros((pad_rows, HEAD_DIM), F32)
    heads = []
    for h in range(KV_HEADS):
        hs = slice(h * HEAD_DIM, (h + 1) * HEAD_DIM)
        vs = slice(KV_W + h * HEAD_DIM, KV_W + (h + 1) * HEAD_DIM)
        ks_sl = slice(2 * KV_W + h * HEAD_DIM, 2 * KV_W + (h + 1) * HEAD_DIM)
        vs_sl = slice(3 * KV_W + h * HEAD_DIM, 3 * KV_W + (h + 1) * HEAD_DIM)
        q_rows = _head_rows(q, h, HEAD_DIM ** -0.5)
        slope = _row_slopes(h, t)
        o_c, p_c = _cmp_branch(q_rows, kc[:, hs], vc[:, hs], slope, qpos, n_cmp_past)
        sel_bf = _select_blocks(p_c, qpos_q, t, n_sel).astype(BF16)

        k_new = jnp.concatenate([kvn[:, ks_sl], zpad], axis=0).astype(BF16)
        v_new = jnp.concatenate([kvn[:, vs_sl], zpad], axis=0).astype(BF16)
        m_lane = jnp.full((rows, KEY_CHUNK), NEG_INF, F32)
        for c in range(n_pages + 1):
            k_c = page_refs[c][:, ks_sl].astype(BF16) if c < n_pages else k_new
            d = qpos - (c * KEY_CHUNK + lane)
            flags = _expand_sel(sel_bf, 2 * c, t)
            s = _qk(q_rows, k_c) - slope * d.astype(F32)
            s = jnp.where((flags > 0.5) & (d >= 0), s, NEG_INF)
            s_scr[c] = s
            m_lane = jnp.maximum(m_lane, s)
        o_s = _finish_softmax(
            s_scr, n_pages + 1, m_lane,
            lambda c: page_refs[c][:, vs_sl].astype(BF16) if c < n_pages else v_new, rows)

        kw_new = jnp.concatenate([kwn[:, hs], zpad], axis=0).astype(BF16)
        vw_new = jnp.concatenate([kwn[:, vs], zpad], axis=0).astype(BF16)
        m_lane = jnp.full((rows, KEY_CHUNK), NEG_INF, F32)
        for c in range(n_win_chunks + 1):
            if c < n_win_chunks:
                k_c = win_ref[c * KEY_CHUNK:(c + 1) * KEY_CHUNK, hs].astype(BF16)
            else:
                k_c = kw_new
            kpos = past - win_len + c * KEY_CHUNK + lane
            d = qpos - kpos
            s = _qk(q_rows, k_c) - slope * d.astype(F32)
            s = jnp.where((d >= 0) & (d <= WINDOW) & (kpos >= 0), s, NEG_INF)
            s_scr[c] = s
            m_lane = jnp.maximum(m_lane, s)
        o_w = _finish_softmax(
            s_scr, n_win_chunks + 1, m_lane,
            lambda c: (win_ref[c * KEY_CHUNK:(c + 1) * KEY_CHUNK, vs].astype(BF16)
                       if c < n_win_chunks else vw_new), rows)
        heads.extend(_gate_mix(gs, h, t, o_c, o_s, o_w))
    o_ref[...] = jnp.concatenate(heads, axis=1).astype(o_ref.dtype)
    win_out_ref[0:win_len - t, :] = win_ref[t:win_len, :]
    win_out_ref[win_len - t:win_len, :] = kwn


def _nsa_sample(page_table, cache4, layer, z, row0, win4, wck2, wcv2, n_seq, t_new):
    n_pages = page_table.shape[1]
    win_len = win4.shape[2]
    rb0 = row0 // t_new
    rows = GROUP * t_new

    def page_spec(p):
        return pl.BlockSpec((None, None, KEY_CHUNK, 4 * KV_W), lambda n, pt: (layer, pt[n, p], 0, 0))

    in_specs = [page_spec(p) for p in range(n_pages)] + [
        pl.BlockSpec((t_new, NSA_W), lambda n, pt: (rb0 + n, C_Q // NSA_W)),
        pl.BlockSpec((t_new, 4 * KV_W), lambda n, pt: (rb0 + n, C_KV4 // (4 * KV_W))),
        pl.BlockSpec((t_new, 2 * KV_W), lambda n, pt: (rb0 + n, C_KVW // (2 * KV_W))),
        pl.BlockSpec((t_new, LANES), lambda n, pt: (rb0 + n, C_GNSA // LANES)),
        pl.BlockSpec((None, None, win_len, 2 * KV_W), lambda n, pt: (layer, n, 0, 0)),
        pl.BlockSpec((CMP_BLOCK, KV_W), lambda n, pt: (0, 0)),
        pl.BlockSpec((CMP_BLOCK, KV_W), lambda n, pt: (0, 0)),
    ]
    grid_spec = pltpu.PrefetchScalarGridSpec(
        num_scalar_prefetch=1,
        grid=(n_seq,),
        in_specs=in_specs,
        out_specs=[pl.BlockSpec((t_new, NSA_W), lambda n, pt: (n, 0)),
                   pl.BlockSpec((None, win_len, 2 * KV_W), lambda n, pt: (n, 0, 0))],
        scratch_shapes=[pltpu.VMEM((LANES, KV_W), F32),
                        pltpu.VMEM((LANES, KV_W), F32),
                        pltpu.VMEM((n_pages + 1, rows, KEY_CHUNK), F32)],
    )
    return pl.pallas_call(
        functools.partial(_nsa_sample_kernel, n_pages=n_pages, t_new=t_new),
        out_shape=[jax.ShapeDtypeStruct((n_seq * t_new, NSA_W), BF16),
                   jax.ShapeDtypeStruct((n_seq, win_len, 2 * KV_W), F32)],
        grid_spec=grid_spec,
        compiler_params=_cparams(("parallel",)),
        name="nsa_sample",
    )(page_table, *([cache4] * n_pages), z, z, z, z, win4, wck2, wcv2)


def _gelu_tanh(x):
    return 0.5 * x * (1.0 + jnp.tanh(np.sqrt(2.0 / np.pi).astype(np.float32) * (x + 0.044715 * (x * x * x))))


def _lru_kernel(x_ref, g_ref, hist_ref, h0_ref, cw_ref, cb_ref, wa_ref, ba_ref, wx_ref, bx_ref, lam_ref,
                y_ref, hlast_ref, histnew_ref, xpad_scr, a_scr, b_scr, hs_scr, h_scr):
    tt = pl.program_id(1)
    nt = pl.num_programs(1)
    rows = x_ref.shape[0]
    k_hist = LRU_CONV - 1

    @pl.when(tt == 0)
    def _():
        xpad_scr[0:SUBLANES, :] = jnp.zeros((SUBLANES, LRU_W), F32)
        xpad_scr[SUBLANES - k_hist:SUBLANES, :] = hist_ref[...]
        h_scr[...] = h0_ref[...]

    x = x_ref[...]
    xpad_scr[SUBLANES:SUBLANES + rows, :] = x
    cw = cw_ref[...]
    xc = cb_ref[...] + cw[k_hist:k_hist + 1] * x
    for k in range(k_hist):
        off = SUBLANES - k_hist + k
        xc = xc + cw[k:k + 1] * xpad_scr[off:off + rows, :]
    xcb = xc.astype(BF16)
    r = _sigmoid(jnp.dot(xcb, wa_ref[...], preferred_element_type=F32) + ba_ref[...])
    ig = _sigmoid(jnp.dot(xcb, wx_ref[...], preferred_element_type=F32) + bx_ref[...])
    nl = -lam_ref[...]
    softplus = jnp.maximum(nl, 0.0) + jnp.log1p(jnp.exp(-jnp.abs(nl)))
    log_a = -LRU_C * r * softplus
    a = jnp.exp(log_a)
    a_scr[...] = a
    b_scr[...] = jnp.sqrt(-jnp.tanh(log_a) * (a * a + 1.0)) * (ig * xc)

    def step(j, h):
        base = pl.multiple_of(j * SUBLANES, SUBLANES)
        a8 = a_scr[pl.ds(base, SUBLANES), :]
        b8 = b_scr[pl.ds(base, SUBLANES), :]
        out = []
        for s in range(SUBLANES):
            h = a8[s:s + 1] * h + b8[s:s + 1]
            out.append(h)
        hs_scr[pl.ds(base, SUBLANES), :] = jnp.concatenate(out, axis=0)
        return h

    h = lax.fori_loop(0, rows // SUBLANES, step, h_scr[...])
    h_scr[...] = h
    y_ref[...] = (hs_scr[...] * _gelu_tanh(g_ref[...])).astype(y_ref.dtype)
    tail = xpad_scr[rows + SUBLANES - k_hist:rows + SUBLANES, :]
    xpad_scr[SUBLANES - k_hist:SUBLANES, :] = tail

    @pl.when(tt == nt - 1)
    def _():
        hlast_ref[...] = h
        histnew_ref[...] = tail


def _lru(z, row0, n_seq, t_len, tile, hist, h0, cw, cb, wa, ba, wx, bx, lam):
    nt = t_len // tile
    rb0 = row0 // tile
    const = lambda shape: pl.BlockSpec(shape, lambda n, tt: (0,) * len(shape))
    return pl.pallas_call(
        _lru_kernel,
        out_shape=[jax.ShapeDtypeStruct((n_seq * t_len, LRU_W), BF16),
                   jax.ShapeDtypeStruct((n_seq, 1, LRU_W), F32),
                   jax.ShapeDtypeStruct((n_seq, LRU_CONV - 1, LRU_W), F32)],
        grid=(n_seq, nt),
        in_specs=[pl.BlockSpec((tile, LRU_W), lambda n, tt: (rb0 + n * nt + tt, C_LRUX // LRU_W)),
                  pl.BlockSpec((tile, LRU_W), lambda n, tt: (rb0 + n * nt + tt, C_LRUG // LRU_W)),
                  pl.BlockSpec((None, LRU_CONV - 1, LRU_W), lambda n, tt: (n, 0, 0)),
                  pl.BlockSpec((None, 1, LRU_W), lambda n, tt: (n, 0, 0)),
                  const((LRU_CONV, LRU_W)), const((1, LRU_W)),
                  const((LRU_W, LRU_W)), const((1, LRU_W)),
                  const((LRU_W, LRU_W)), const((1, LRU_W)), const((1, LRU_W))],
        out_specs=[pl.BlockSpec((tile, LRU_W), lambda n, tt: (n * nt + tt, 0)),
                   pl.BlockSpec((None, 1, LRU_W), lambda n, tt: (n, 0, 0)),
                   pl.BlockSpec((None, LRU_CONV - 1, LRU_W), lambda n, tt: (n, 0, 0))],
        scratch_shapes=[pltpu.VMEM((tile + SUBLANES, LRU_W), F32),
                        pltpu.VMEM((tile, LRU_W), F32),
                        pltpu.VMEM((tile, LRU_W), F32),
                        pltpu.VMEM((tile, LRU_W), F32),
                        pltpu.VMEM((1, LRU_W), F32)],
        compiler_params=_cparams(("parallel", "arbitrary")),
        name="rglru",
    )(z, z, hist, h0, cw, cb, wa, ba, wx, bx, lam)


CMF_HIST = CMF_CONV - 1
CMF_PAD = 32
CMF_SUB = 32


def _cmf_kernel(a_ref, g_ref, hist_ref, w_ref, b_ref, lng_ref, lnb_ref, y_ref, histnew_ref, upad_scr):
    tt = pl.program_id(1)
    nt = pl.num_programs(1)
    rows = a_ref.shape[0]
    lo = CMF_PAD - CMF_HIST

    @pl.when(tt == 0)
    def _():
        upad_scr[0:CMF_PAD, :] = jnp.zeros((CMF_PAD, CMF_W), F32)
        upad_scr[lo:CMF_PAD, :] = hist_ref[...]

    upad_scr[CMF_PAD:CMF_PAD + rows, :] = a_ref[...] * _sigmoid(g_ref[...])
    w = w_ref[...]
    sub = min(CMF_SUB, rows)
    for r0 in range(0, rows, sub):
        c = b_ref[...] + w[0:1] * upad_scr[lo + r0:lo + r0 + sub, :]
        for k in range(1, CMF_CONV):
            c = c + w[k:k + 1] * upad_scr[lo + r0 + k:lo + r0 + k + sub, :]
        mu = jnp.mean(c, axis=-1, keepdims=True)
        cc = c - mu
        var = jnp.mean(cc * cc, axis=-1, keepdims=True)
        c = cc * lax.rsqrt(var + EPS) * lng_ref[...] + lnb_ref[...]
        y_ref[r0:r0 + sub, :] = (c * _sigmoid(c)).astype(y_ref.dtype)
    tail = upad_scr[rows + lo:rows + CMF_PAD, :]
    upad_scr[lo:CMF_PAD, :] = tail

    @pl.when(tt == nt - 1)
    def _():
        histnew_ref[...] = tail


def _cmf(z, row0, n_seq, t_len, tile, hist, w, b, lng, lnb):
    nt = t_len // tile
    rb0 = row0 // tile
    const = lambda shape: pl.BlockSpec(shape, lambda n, tt: (0,) * len(shape))
    return pl.pallas_call(
        _cmf_kernel,
        out_shape=[jax.ShapeDtypeStruct((n_seq * t_len, CMF_W), BF16),
                   jax.ShapeDtypeStruct((n_seq, CMF_HIST, CMF_W), F32)],
        grid=(n_seq, nt),
        in_specs=[pl.BlockSpec((tile, CMF_W), lambda n, tt: (rb0 + n * nt + tt, C_CMFA // CMF_W)),
                  pl.BlockSpec((tile, CMF_W), lambda n, tt: (rb0 + n * nt + tt, C_CMFG // CMF_W)),
                  pl.BlockSpec((None, CMF_HIST, CMF_W), lambda n, tt: (n, 0, 0)),
                  const((CMF_CONV, CMF_W)), const((1, CMF_W)), const((1, CMF_W)), const((1, CMF_W))],
        out_specs=[pl.BlockSpec((tile, CMF_W), lambda n, tt: (n * nt + tt, 0)),
                   pl.BlockSpec((None, CMF_HIST, CMF_W), lambda n, tt: (n, 0, 0))],
        scratch_shapes=[pltpu.VMEM((tile + CMF_PAD, CMF_W), F32)],
        compiler_params=_cparams(("parallel", "arbitrary")),
        name="conformer_conv",
    )(z, z, hist, w, b, lng, lnb)


def _merge_kernel(*refs, with_router):
    if with_router:
        (h_ref, g0_ref, g1_ref, g2_ref, on_ref, yl_ref, yc_ref, wn_ref, wl_ref, wc_ref, wo_ref, gf_ref, wr_ref,
         hout_ref, v_ref, route_ref) = refs
    else:
        (h_ref, g0_ref, g1_ref, g2_ref, on_ref, yl_ref, yc_ref, wn_ref, wl_ref, wc_ref, wo_ref, gf_ref,
         hout_ref, v_ref) = refs
    merged = (_sigmoid(g0_ref[...]) * jnp.dot(on_ref[...], wn_ref[...], preferred_element_type=F32)
              + _sigmoid(g1_ref[...]) * jnp.dot(yl_ref[...], wl_ref[...], preferred_element_type=F32)
              + _sigmoid(g2_ref[...]) * jnp.dot(yc_ref[...], wc_ref[...], preferred_element_type=F32))
    h = h_ref[...] + jnp.dot(merged.astype(BF16), wo_ref[...], preferred_element_type=F32)
    hout_ref[...] = h
    v = _rms(h, gf_ref[...]).astype(BF16)
    v_ref[...] = v
    if with_router:
        logits = jnp.dot(v, wr_ref[...], preferred_element_type=F32)
        col = lax.broadcasted_iota(I32, logits.shape, 1)
        colf = col.astype(F32)
        live = col < N_EXPERTS
        l1 = jnp.max(jnp.where(live, logits, NEG_INF), axis=-1, keepdims=True)
        i1 = jnp.min(jnp.where(live & (logits == l1), colf, float(LANES)), axis=-1, keepdims=True)
        rest = live & (colf != i1)
        l2 = jnp.max(jnp.where(rest, logits, NEG_INF), axis=-1, keepdims=True)
        i2 = jnp.min(jnp.where(rest & (logits == l2), colf, float(LANES)), axis=-1, keepdims=True)
        e2 = jnp.exp(l2 - l1)
        w1 = 1.0 / (1.0 + e2)
        w2 = e2 / (1.0 + e2)
        route_ref[...] = jnp.where(col == 0, w1, jnp.where(col == 1, w2, jnp.where(
            col == 2, i1, jnp.where(col == 3, i2, 0.0))))


def _merge(h, z, o_nsa, y_lru, y_cmf, wn, wl, wc, wo, gf, wr, tm):
    m, d = h.shape
    with_router = wr is not None
    gm_blk = C_GM // d
    const = lambda shape: pl.BlockSpec(shape, lambda i: (0, 0))
    in_specs = [pl.BlockSpec((tm, d), lambda i: (i, 0)),
                pl.BlockSpec((tm, d), lambda i: (i, gm_blk)),
                pl.BlockSpec((tm, d), lambda i: (i, gm_blk + 1)),
                pl.BlockSpec((tm, d), lambda i: (i, gm_blk + 2)),
                pl.BlockSpec((tm, NSA_W), lambda i: (i, 0)),
                pl.BlockSpec((tm, LRU_W), lambda i: (i, 0)),
                pl.BlockSpec((tm, CMF_W), lambda i: (i, 0)),
                const((NSA_W, d)), const((LRU_W, d)), const((CMF_W, d)), const((d, d)), const((1, d))]
    args = [h, z, z, z, o_nsa, y_lru, y_cmf, wn, wl, wc, wo, gf]
    out_shape = [jax.ShapeDtypeStruct((m, d), F32), jax.ShapeDtypeStruct((m, d), BF16)]
    out_specs = [pl.BlockSpec((tm, d), lambda i: (i, 0)), pl.BlockSpec((tm, d), lambda i: (i, 0))]
    if with_router:
        in_specs.append(const((d, LANES)))
        args.append(wr)
        out_shape.append(jax.ShapeDtypeStruct((m, LANES), F32))
        out_specs.append(pl.BlockSpec((tm, LANES), lambda i: (i, 0)))
    return pl.pallas_call(
        functools.partial(_merge_kernel, with_router=with_router),
        out_shape=out_shape,
        grid=(m // tm,),
        in_specs=in_specs,
        out_specs=out_specs,
        compiler_params=_cparams(("parallel",)),
        name="merge_out",
    )(*args)


def _ffn_kernel(v_ref, h_ref, wg_ref, wu_ref, wd_ref, o_ref, acc_scr):
    j = pl.program_id(1)

    @pl.when(j == 0)
    def _():
        acc_scr[...] = h_ref[...]

    v = v_ref[...]
    g = jnp.dot(v, wg_ref[...], preferred_element_type=F32)
    u = jnp.dot(v, wu_ref[...], preferred_element_type=F32)
    a = (g * _sigmoid(g) * u).astype(BF16)
    acc_scr[...] += jnp.dot(a, wd_ref[...], preferred_element_type=F32)

    @pl.when(j == pl.num_programs(1) - 1)
    def _():
        o_ref[...] = acc_scr[...]


def _ffn(v, h, wg, wu, wd, tm, tf):
    m, d = h.shape
    f = wg.shape[1]
    return pl.pallas_call(
        _ffn_kernel,
        out_shape=jax.ShapeDtypeStruct((m, d), F32),
        grid=(m // tm, f // tf),
        in_specs=[pl.BlockSpec((tm, d), lambda i, j: (i, 0)),
                  pl.BlockSpec((tm, d), lambda i, j: (i, 0)),
                  pl.BlockSpec((d, tf), lambda i, j: (0, j)),
                  pl.BlockSpec((d, tf), lambda i, j: (0, j)),
                  pl.BlockSpec((tf, d), lambda i, j: (j, 0))],
        out_specs=pl.BlockSpec((tm, d), lambda i, j: (i, 0)),
        scratch_shapes=[pltpu.VMEM((tm, d), F32)],
        compiler_params=_cparams(("parallel", "arbitrary")),
        name="ffn_dense",
    )(v, h, wg, wu, wd)


def _moe_kernel(te_ref, nu_ref, x_ref, cw_ref, wg_ref, wu_ref, wd_ref, y_ref, acc_scr):
    i = pl.program_id(0)
    j = pl.program_id(1)
    del te_ref
    used = i < nu_ref[0]

    @pl.when(j == 0)
    def _():
        acc_scr[...] = jnp.zeros(acc_scr.shape, F32)

    @pl.when(used)
    def _():
        x = x_ref[...]
        g = jnp.dot(x, wg_ref[...], preferred_element_type=F32)
        u = jnp.dot(x, wu_ref[...], preferred_element_type=F32)
        a = (g * _sigmoid(g) * u).astype(BF16)
        acc_scr[...] += jnp.dot(a, wd_ref[...], preferred_element_type=F32)

    @pl.when(j == pl.num_programs(1) - 1)
    def _():
        y_ref[...] = acc_scr[...] * cw_ref[...]


def _moe_grouped(tile_expert, n_used, x_sorted, cw_sorted, wg, wu, wd, tm, tf):
    mp, d = x_sorted.shape
    f = wg.shape[2]
    grid_spec = pltpu.PrefetchScalarGridSpec(
        num_scalar_prefetch=2,
        grid=(mp // tm, f // tf),
        in_specs=[pl.BlockSpec((tm, d), lambda i, j, te, nu: (i, 0)),
                  pl.BlockSpec((tm, 1), lambda i, j, te, nu: (i, 0)),
                  pl.BlockSpec((None, d, tf), lambda i, j, te, nu: (te[i], 0, j)),
                  pl.BlockSpec((None, d, tf), lambda i, j, te, nu: (te[i], 0, j)),
                  pl.BlockSpec((None, tf, d), lambda i, j, te, nu: (te[i], j, 0))],
        out_specs=pl.BlockSpec((tm, d), lambda i, j, te, nu: (i, 0)),
        scratch_shapes=[pltpu.VMEM((tm, d), F32)],
    )
    return pl.pallas_call(
        _moe_kernel,
        out_shape=jax.ShapeDtypeStruct((mp, d), F32),
        grid_spec=grid_spec,
        compiler_params=_cparams(("parallel", "arbitrary")),
        name="moe_experts",
    )(tile_expert, n_used, x_sorted, cw_sorted, wg, wu, wd)


def _moe(v, h, route, wg, wu, wd, tm, tf):
    m, d = h.shape
    n_pairs = TOP_E * m
    mp = n_pairs + N_EXPERTS * tm
    n_tiles = mp // tm
    idx = route[:, 2:2 + TOP_E].astype(I32).reshape(-1)
    cw = route[:, 0:TOP_E].reshape(-1)
    order = jnp.argsort(idx, stable=True)
    counts = jnp.zeros((N_EXPERTS,), I32).at[idx].add(1)
    padded = ((counts + tm - 1) // tm) * tm
    pad_end = jnp.cumsum(padded)
    group_start = pad_end - padded
    count_start = jnp.cumsum(counts) - counts
    sorted_e = idx[order]
    dest = group_start[sorted_e] + (jnp.arange(n_pairs, dtype=I32) - count_start[sorted_e])
    tok_of_pos = jnp.zeros((mp,), I32).at[dest].set((order // TOP_E).astype(I32))
    cw_of_pos = jnp.zeros((mp,), F32).at[dest].set(cw[order])
    pos_of_pair = jnp.zeros((n_pairs,), I32).at[order].set(dest).reshape(m, TOP_E)
    tile_expert = jnp.minimum(
        jnp.searchsorted(pad_end, jnp.arange(n_tiles, dtype=I32) * tm, side="right"), N_EXPERTS - 1).astype(I32)
    n_used = (pad_end[-1] // tm).astype(I32).reshape(1)
    x_sorted = jnp.take(v, tok_of_pos, axis=0)
    y = _moe_grouped(tile_expert, n_used, x_sorted, cw_of_pos.reshape(mp, 1), wg, wu, wd, tm, tf)
    out = h
    for k in range(TOP_E):
        out = out + jnp.take(y, pos_of_pair[:, k], axis=0)
    return out


def _norm_kernel(h_ref, g_ref, o_ref):
    o_ref[...] = _rms(h_ref[...], g_ref[...])


def _final_norm(h, g, tm):
    m, d = h.shape
    return pl.pallas_call(
        _norm_kernel,
        out_shape=jax.ShapeDtypeStruct((m, d), F32),
        grid=(m // tm,),
        in_specs=[pl.BlockSpec((tm, d), lambda i: (i, 0)), pl.BlockSpec((1, d), lambda i: (0, 0))],
        out_specs=pl.BlockSpec((tm, d), lambda i: (i, 0)),
        compiler_params=_cparams(("parallel",)),
        name="final_norm",
    )(h, g)


def _layout_w_in(w):
    d = w.shape[0]
    o = 0
    parts = {}
    for name, width in (("q", NSA_W), ("kv4", 4 * KV_W), ("kvw", 2 * KV_W), ("gn", N_BRANCH * N_HEADS),
                        ("lx", LRU_W), ("lg", LRU_W), ("ca", CMF_W), ("cg", CMF_W), ("gm", N_BRANCH * D_MODEL)):
        parts[name] = w[:, o:o + width]
        o += width
    used = C_GNSA + N_BRANCH * N_HEADS
    cols = [parts[k] for k in ("q", "kv4", "lx", "lg", "ca", "cg", "gm", "kvw", "gn")]
    cols.append(jnp.zeros((d, N_Z - used), w.dtype))
    return jnp.concatenate(cols, axis=1).astype(BF16)


def _block_diag(w):
    eye = jnp.eye(LRU_BLOCKS, dtype=w.dtype)
    return jnp.einsum("kcd,kj->kcjd", w, eye).reshape(LRU_W, LRU_W).astype(BF16)


TM_INPROJ, TN_INPROJ = 512, 3328
TM_MERGE = 256
TM_FFN, TF_FFN = 512, 1408
TM_MOE, TF_MOE = 512, 1792
LRU_TILE = 256
CMF_TILE = 256


def kernel(x_prompt, x_sample, cache_nsa_kv, cache_win_kv, state_lru_h, state_lru_conv, state_cmf_conv,
           page_table, g_mix, w_in, w_cmp_k, w_cmp_v, lru_conv_w, lru_conv_b, lru_w_a, lru_b_a, lru_w_x,
           lru_b_x, lru_lambda, cmf_dw_w, cmf_dw_b, cmf_ln_g, cmf_ln_b, w_br_nsa, w_br_lru, w_br_cmf, w_out,
           g_ffn, ffn_w_gate, ffn_w_up, ffn_w_down, moe_router, moe_w_gate, moe_w_up, moe_w_down, g_final):
    batch, seq, d = x_prompt.shape
    n_seq, t_new, _ = x_sample.shape
    depth = w_in.shape[0]
    mp_rows = batch * seq
    ms_rows = n_seq * t_new
    win_len = cache_win_kv.shape[2]
    page = cache_nsa_kv.shape[2]
    assert page == KEY_CHUNK and seq % Q_BLOCK == 0 and win_len % KEY_CHUNK == 0
    assert t_new % SUBLANES == 0 and t_new <= KEY_CHUNK

    cache4 = cache_nsa_kv.reshape(depth, cache_nsa_kv.shape[1], page, 4 * KV_W)
    win4 = cache_win_kv.reshape(depth, n_seq, win_len, 2 * KV_W)
    h = jnp.concatenate([x_prompt.reshape(mp_rows, d), x_sample.reshape(ms_rows, d)], axis=0)
    row = lambda a: a.reshape(1, -1)
    zeros_p = lambda *s: jnp.zeros((batch,) + s, F32)

    out = {k: [] for k in ("kv_p", "kv_s", "win_p", "win_s", "lh_p", "lh_s", "lc_p", "lc_s", "cc_p", "cc_s")}
    for l in range(depth):
        z = _inproj(h, row(g_mix[l]), _layout_w_in(w_in[l]), TM_INPROJ, TN_INPROJ)
        wck2 = jnp.concatenate([w_cmp_k[l]] * KV_HEADS, axis=1)
        wcv2 = jnp.concatenate([w_cmp_v[l]] * KV_HEADS, axis=1)
        o_p = _nsa_prompt(z, wck2, wcv2, batch, seq)
        o_s, win_s = _nsa_sample(page_table, cache4, l, z, mp_rows, win4, wck2, wcv2, n_seq, t_new)

        lru_args = (lru_conv_w[l], row(lru_conv_b[l]), _block_diag(lru_w_a[l]), row(lru_b_a[l]),
                    _block_diag(lru_w_x[l]), row(lru_b_x[l]), row(lru_lambda[l]))
        yl_p, lh_p, lc_p = _lru(z, 0, batch, seq, LRU_TILE, zeros_p(LRU_CONV - 1, LRU_W), zeros_p(1, LRU_W),
                                *lru_args)
        yl_s, lh_s, lc_s = _lru(z, mp_rows, n_seq, t_new, t_new, state_lru_conv[l],
                                state_lru_h[l].reshape(n_seq, 1, LRU_W), *lru_args)
        cmf_args = (cmf_dw_w[l], row(cmf_dw_b[l]), row(cmf_ln_g[l]), row(cmf_ln_b[l]))
        yc_p, cc_p = _cmf(z, 0, batch, seq, CMF_TILE, zeros_p(CMF_HIST, CMF_W), *cmf_args)
        yc_s, cc_s = _cmf(z, mp_rows, n_seq, t_new, t_new, state_cmf_conv[l], *cmf_args)

        is_moe = l % 2 == 1
        j = l // 2
        wr = None
        if is_moe:
            wr = jnp.pad(moe_router[j], ((0, 0), (0, LANES - N_EXPERTS))).astype(BF16)
        res = _merge(h, z, jnp.concatenate([o_p, o_s], axis=0), jnp.concatenate([yl_p, yl_s], axis=0),
                     jnp.concatenate([yc_p, yc_s], axis=0), w_br_nsa[l].astype(BF16), w_br_lru[l].astype(BF16),
                     w_br_cmf[l].astype(BF16), w_out[l].astype(BF16), row(g_ffn[l]), wr, TM_MERGE)
        if is_moe:
            h, v, route = res
            h = _moe(v, h, route, moe_w_gate[j].astype(BF16), moe_w_up[j].astype(BF16),
                     moe_w_down[j].astype(BF16), TM_MOE, TF_MOE)
        else:
            h, v = res
            h = _ffn(v, h, ffn_w_gate[j].astype(BF16), ffn_w_up[j].astype(BF16), ffn_w_down[j].astype(BF16),
                     TM_FFN, TF_FFN)

        kv4 = z[:, C_KV4:C_KV4 + 4 * KV_W]
        out["kv_p"].append(kv4[:mp_rows].reshape(batch, seq, 4, KV_HEADS, HEAD_DIM))
        out["kv_s"].append(kv4[mp_rows:].reshape(n_seq, t_new, 4, KV_HEADS, HEAD_DIM))
        kvw_p = z[:mp_rows, C_KVW:C_KVW + 2 * KV_W].reshape(batch, seq, 2, KV_HEADS, HEAD_DIM)
        out["win_p"].append(kvw_p[:, -min(WINDOW, seq):])
        out["win_s"].append(win_s.reshape(n_seq, win_len, 2, KV_HEADS, HEAD_DIM))
        out["lh_p"].append(lh_p.reshape(batch, LRU_W))
        out["lh_s"].append(lh_s.reshape(n_seq, LRU_W))
        out["lc_p"].append(lc_p)
        out["lc_s"].append(lc_s)
        out["cc_p"].append(cc_p)
        out["cc_s"].append(cc_s)

    y = _final_norm(h, row(g_final), TM_MERGE)
    st = lambda k: jnp.stack(out[k])
    return (y[:mp_rows].reshape(batch, seq, d), y[mp_rows:].reshape(n_seq, t_new, d),
            st("kv_p"), st("kv_s"), st("win_p"), st("win_s"), st("lh_p"), st("lh_s"),
            st("lc_p"), st("lc_s"), st("cc_p"), st("cc_s"))
```
